```python
import jax, jax.numpy as jnp
from jax import lax
import numpy as np

D_MODEL = 1024
BATCH = 2
SEQ = 8192
DEPTH = 1

HEAD_DIM = 64
A_Q_HEADS = 6
A_KV_HEADS = 2
A_GROUPS = A_Q_HEADS // A_KV_HEADS
B_Q_HEADS = 6
B_KV_HEADS = 2
B_GROUPS = B_Q_HEADS // B_KV_HEADS
M_HEADS = 4
N_MEM = 256
GRID_W = 64
Q_BLOCK = 128
WINDOW = 128
ROPE_THETA = 10000.0
N_BRANCHES = 3
N_EXPERTS = 32
TOP_K = 4
D_EXPERT = D_MODEL
SWIGLU_ALPHA = 1.702
SWIGLU_LIMIT = 7.0
MOE_BLOCK = 128
EPS = 1e-6

A_Q = A_Q_HEADS * HEAD_DIM
A_KV = A_KV_HEADS * HEAD_DIM
B_Q = B_Q_HEADS * HEAD_DIM
B_KV = B_KV_HEADS * HEAD_DIM
M_Q = M_HEADS * HEAD_DIM
GATE_W = N_BRANCHES * D_MODEL
IN_WIDTH = A_Q + 2 * A_KV + B_Q + 2 * B_KV + M_Q + GATE_W

kernel_name = "hybrid_gated_gqa_window_memory_moe_encoder"


def rmsnorm(x, g):
    xf = x.astype(jnp.float32)
    y = xf * lax.rsqrt(jnp.mean(xf * xf, axis=-1, keepdims=True) + EPS)
    return (y * g.astype(jnp.float32)).astype(x.dtype)


def rope(x, pos):
    d = x.shape[-1]
    inv = ROPE_THETA ** (-jnp.arange(0, d, 2, dtype=jnp.float32) / d)
    ang = pos.astype(jnp.float32)[:, None] * inv[None, :]
    cos, sin = jnp.cos(ang), jnp.sin(ang)
    xf = x.astype(jnp.float32)
    x1, x2 = xf[..., : d // 2], xf[..., d // 2:]
    return jnp.concatenate([x1 * cos - x2 * sin, x2 * cos + x1 * sin], axis=-1).astype(x.dtype)


def axial_rope(x, rows, cols):
    half = x.shape[-1] // 2
    return jnp.concatenate([rope(x[..., :half], rows), rope(x[..., half:], cols)], axis=-1)


def heads_q(t, kv, g):
    b, s, _ = t.shape
    return t.reshape(b, s, kv, g, HEAD_DIM).transpose(0, 2, 3, 1, 4)


def heads_kv(t, kv):
    b, s, _ = t.shape
    return t.reshape(b, s, kv, HEAD_DIM).transpose(0, 2, 1, 3)


def merge_heads(o):
    b, kv, g, s, d = o.shape
    return o.transpose(0, 3, 1, 2, 4).reshape(b, s, kv * g * d)


def global_gqa(q, k, v):
    b, kv, g, s, d = q.shape
    nb = s // Q_BLOCK
    scale = d ** -0.5
    qb = jnp.moveaxis(q.reshape(b, kv, g, nb, Q_BLOCK, d), 3, 0)

    def block(qi):
        sc = jnp.einsum('bkgqd,bksd->bkgqs', qi, k).astype(jnp.float32) * scale
        p = jax.nn.softmax(sc, axis=-1).astype(v.dtype)
        return jnp.einsum('bkgqs,bksd->bkgqd', p, v)

    o = lax.map(block, qb)
    return jnp.moveaxis(o, 0, 3).reshape(b, kv, g, s, d)


def window_gqa_sink(q, k, v, sink):
    b, kv, g, s, d = q.shape
    nb = s // Q_BLOCK
    scale = d ** -0.5
    pad = ((0, 0), (0, 0), (Q_BLOCK, Q_BLOCK), (0, 0))
    kp = jnp.pad(k, pad).reshape(b, kv, nb + 2, Q_BLOCK, d)
    vp = jnp.pad(v, pad).reshape(b, kv, nb + 2, Q_BLOCK, d)
    band = lambda t: jnp.concatenate([t[:, :, :-2], t[:, :, 1:-1], t[:, :, 2:]], axis=3)
    kw, vw = band(kp), band(vp)
    qb = q.reshape(b, kv, g, nb, Q_BLOCK, d)
    sc = jnp.einsum('bkgnqd,bknsd->bkgnqs', qb, kw).astype(jnp.float32) * scale
    blk = jnp.arange(nb)[:, None, None] * Q_BLOCK
    qpos = blk + jnp.arange(Q_BLOCK)[None, :, None]
    kpos = blk - Q_BLOCK + jnp.arange(3 * Q_BLOCK)[None, None, :]
    valid = (jnp.abs(kpos - qpos) <= WINDOW) & (kpos >= 0) & (kpos < s)
    sc = jnp.where(valid, sc, jnp.finfo(jnp.float32).min)
    sink_col = jnp.broadcast_to(sink.astype(jnp.float32).reshape(1, kv, g, 1, 1, 1), sc.shape[:-1] + (1,))
    p = jax.nn.softmax(jnp.concatenate([sc, sink_col], axis=-1), axis=-1)[..., :-1]
    o = jnp.einsum('bkgnqs,bknsd->bkgnqd', p.astype(v.dtype), vw)
    return o.reshape(b, kv, g, s, d)


def memory_attention(q, km, vm):
    scale = q.shape[-1] ** -0.5
    sc = jnp.einsum('bhsd,bhnd->bhsn', q, km).astype(jnp.float32) * scale
    p = jax.nn.softmax(sc, axis=-1).astype(vm.dtype)
    return jnp.einsum('bhsn,bhnd->bhsd', p, vm)


def moe(h, w_router, b_router, w_gu, b_gu, w_down, b_down):
    bn, s, d = h.shape
    t = bn * s
    ht = h.reshape(t, d)
    logits = (ht @ w_router + b_router).astype(jnp.float32)
    top_v, top_i = lax.top_k(logits, TOP_K)
    gate = jax.nn.softmax(top_v, axis=-1)
    a = t * TOP_K
    flat_e = top_i.reshape(a)
    flat_tok = jnp.repeat(jnp.arange(t, dtype=jnp.int32), TOP_K)
    flat_w = gate.reshape(a)
    order = jnp.argsort(flat_e)
    se = flat_e[order]
    counts = jnp.bincount(flat_e, length=N_EXPERTS)
    padded = (counts + MOE_BLOCK - 1) // MOE_BLOCK * MOE_BLOCK
    start = jnp.cumsum(counts) - counts
    pend = jnp.cumsum(padded)
    pstart = pend - padded
    dest = pstart[se] + jnp.arange(a) - start[se]
    p_rows = a + N_EXPERTS * MOE_BLOCK
    nb = p_rows // MOE_BLOCK
    row_tok = jnp.zeros((p_rows,), jnp.int32).at[dest].set(flat_tok[order])
    row_w = jnp.zeros((p_rows,), jnp.float32).at[dest].set(flat_w[order])
    blk_e = jnp.minimum(jnp.searchsorted(pend, jnp.arange(nb) * MOE_BLOCK, side='right'), N_EXPERTS - 1)
    xs = ht[row_tok].reshape(nb, MOE_BLOCK, d)

    def expert_block(args):
        xb, e = args
        gu = xb @ w_gu[e] + b_gu[e]
        g = jnp.minimum(gu[:, :D_EXPERT], SWIGLU_LIMIT)
        lin = jnp.clip(gu[:, D_EXPERT:], -SWIGLU_LIMIT, SWIGLU_LIMIT)
        act = g * jax.nn.sigmoid(SWIGLU_ALPHA * g) * (lin + 1.0)
        return act @ w_down[e] + b_down[e]

    ys = lax.map(expert_block, (xs, blk_e)).reshape(p_rows, d)
    y = jax.ops.segment_sum(ys * row_w[:, None].astype(ys.dtype), row_tok, num_segments=t)
    return y.reshape(bn, s, d)


def setup_inputs(seed: int = 0) -> dict:
    key = jax.random.key(seed)
    ks = jax.random.split(key, 24)
    f32 = jnp.float32
    w = lambda k, shape, fan_in: jax.random.normal(k, shape, f32) * fan_in ** -0.5
    gain = lambda k, shape: 1.0 + 0.02 * jax.random.normal(k, shape, f32)
    small = lambda k, shape, sc: sc * jax.random.normal(k, shape, f32)
    L = DEPTH
    return {
        "x": jax.random.normal(ks[0], (BATCH, SEQ, D_MODEL), f32),
        "mem": jax.random.normal(ks[1], (BATCH, N_MEM, D_MODEL), f32),
        "norm_mix": gain(ks[2], (L, D_MODEL)),
        "w_in": w(ks[3], (L, D_MODEL, IN_WIDTH), D_MODEL),
        "q_norm_a": gain(ks[4], (L, HEAD_DIM)),
        "k_norm_a": gain(ks[5], (L, HEAD_DIM)),
        "sink_b": small(ks[6], (L, B_Q_HEADS), 0.5),
        "norm_mem": gain(ks[7], (L, D_MODEL)),
        "w_mem_kv": w(ks[8], (L, D_MODEL, 2 * M_Q), D_MODEL),
        "w_branch_a": w(ks[9], (L, A_Q, D_MODEL), A_Q),
        "w_branch_b": w(ks[10], (L, B_Q, D_MODEL), B_Q),
        "w_branch_m": w(ks[11], (L, M_Q, D_MODEL), M_Q),
        "w_out": w(ks[12], (L, D_MODEL, D_MODEL), D_MODEL),
        "norm_ffn": gain(ks[13], (L, D_MODEL)),
        "w_router": w(ks[14], (L, D_MODEL, N_EXPERTS), D_MODEL),
        "b_router": small(ks[15], (L, N_EXPERTS), 0.01),
        "w_gate_up": w(ks[16], (L, N_EXPERTS, D_MODEL, 2 * D_EXPERT), D_MODEL),
        "b_gate_up": small(ks[17], (L, N_EXPERTS, 2 * D_EXPERT), 0.02),
        "w_down": w(ks[18], (L, N_EXPERTS, D_EXPERT, D_MODEL), D_EXPERT),
        "b_down": small(ks[19], (L, N_EXPERTS, D_MODEL), 0.02),
        "norm_final": gain(ks[20], (D_MODEL,)),
    }


def reference(x, mem, norm_mix, w_in, q_norm_a, k_norm_a, sink_b, norm_mem, w_mem_kv,
              w_branch_a, w_branch_b, w_branch_m, w_out, norm_ffn, w_router, b_router,
              w_gate_up, b_gate_up, w_down, b_down, norm_final):
    bn, s, d = x.shape
    n_rows = s // GRID_W
    pos = jnp.arange(s, dtype=jnp.int32)
    rows = jnp.repeat(jnp.arange(n_rows, dtype=jnp.int32), GRID_W, total_repeat_length=s)
    cols = pos % GRID_W
    widths = [A_Q, A_KV, A_KV, B_Q, B_KV, B_KV, M_Q]
    cuts, acc = [], 0
    for wd in widths:
        acc += wd
        cuts.append(acc)
    for l in range(DEPTH):
        h = rmsnorm(x, norm_mix[l])
        proj = h @ w_in[l]
        q_a, k_a, v_a, q_b, k_b, v_b, q_m, gates = jnp.split(proj, cuts, axis=-1)
        q_a = axial_rope(rmsnorm(heads_q(q_a, A_KV_HEADS, A_GROUPS), q_norm_a[l]), rows, cols)
        k_a = axial_rope(rmsnorm(heads_kv(k_a, A_KV_HEADS), k_norm_a[l]), rows, cols)
        y_a = merge_heads(global_gqa(q_a, k_a, heads_kv(v_a, A_KV_HEADS)))
        q_b = rope(heads_q(q_b, B_KV_HEADS, B_GROUPS), pos)
        k_b = rope(heads_kv(k_b, B_KV_HEADS), pos)
        y_b = merge_heads(window_gqa_sink(q_b, k_b, heads_kv(v_b, B_KV_HEADS), sink_b[l]))
        kv_m = rmsnorm(mem, norm_mem[l]) @ w_mem_kv[l]
        k_m, v_m = jnp.split(kv_m, 2, axis=-1)
        o_m = memory_attention(heads_kv(q_m, M_HEADS), heads_kv(k_m, M_HEADS), heads_kv(v_m, M_HEADS))
        y_m = o_m.transpose(0, 2, 1, 3).reshape(bn, s, M_Q)
        g_a, g_b, g_m = jnp.split(jax.nn.sigmoid(gates), N_BRANCHES, axis=-1)
        merged = g_a * (y_a @ w_branch_a[l]) + g_b * (y_b @ w_branch_b[l]) + g_m * (y_m @ w_branch_m[l])
        x = x + merged @ w_out[l]
        x = x + moe(rmsnorm(x, norm_ffn[l]), w_router[l], b_router[l], w_gate_up[l], b_gate_up[l],
                    w_down[l], b_down[l])
    return rmsnorm(x, norm_final)
```

```python
import functools

import jax
import jax.numpy as jnp
from jax import lax
from jax.experimental import pallas as pl
from jax.experimental.pallas import tpu as pltpu

F32 = jnp.float32
BF16 = jnp.bfloat16

HEAD_DIM = 64
LANES = 128
A_HEADS, A_KV = 6, 2
B_HEADS, B_KV = 6, 2
M_HEADS = 4
GRID_W = 64
WINDOW = 128
ROPE_THETA = 10000.0
N_EXPERTS = 32
TOP_K = 4
SWIGLU_ALPHA = 1.702
SWIGLU_LIMIT = 7.0
EPS = 1e-6
NEG = float(jnp.finfo(jnp.float32).min)

VMEM_LIMIT = 56 * 1024 * 1024
CAST_ROWS = 64


def _cparams(*sem):
    return pltpu.CompilerParams(dimension_semantics=sem, vmem_limit_bytes=VMEM_LIMIT)


def _rms(x, gain):
    return x * lax.rsqrt(jnp.mean(x * x, axis=-1, keepdims=True) + EPS) * gain


def _mem_kv_kernel(mem_ref, g_ref, w_ref, o_ref):
    h = _rms(mem_ref[...], g_ref[...]).astype(BF16)
    o_ref[...] = jnp.dot(h, w_ref[...], preferred_element_type=F32).astype(BF16)


def _mem_kv(mem2d, gain, w):
    n, d = mem2d.shape
    return pl.pallas_call(
        _mem_kv_kernel,
        out_shape=jax.ShapeDtypeStruct((n, w.shape[1]), BF16),
        name="mem_kv",
    )(mem2d, gain, w)


QA_W = A_HEADS * LANES
QB_W = B_HEADS * LANES
QM_W = M_HEADS * LANES


def _rot_half(xb, half, first):
    return jnp.where(first, pltpu.roll(xb, LANES - half, 1), pltpu.roll(xb, half, 1))


def _in_proj_kernel(x_ref, gmix_ref, w_ref, cosa_ref, sina_ref, cosb_ref, sinb_ref, qg_ref, kg_ref,
                    qa_o, ka_o, va_o, qb_o, kb_o, vb_o, qm_o, g_o):
    h = _rms(x_ref[...], gmix_ref[...]).astype(BF16)
    lane = lax.broadcasted_iota(jnp.int32, (1, LANES), 1)
    first_a = (lane % 32) < 16
    first_b = (lane % 64) < 32
    low = lane < HEAD_DIM
    scale = HEAD_DIM ** -0.5
    cosa, sina = cosa_ref[...], sina_ref[...]
    cosb, sinb = cosb_ref[...], sinb_ref[...]

    def seg(lo, width):
        return jnp.dot(h, w_ref[:, lo:lo + width], preferred_element_type=F32)

    def rope_a(xb):
        return xb * cosa + _rot_half(xb, 16, first_a) * sina

    def rope_b(xb):
        return xb * cosb + _rot_half(xb, 32, first_b) * sinb

    col = 0
    qa = seg(col, QA_W)
    col += QA_W
    for hd in range(A_HEADS):
        xb = qa[:, hd * LANES:(hd + 1) * LANES]
        ssq = jnp.sum(xb * xb, axis=-1, keepdims=True)
        xb = xb * lax.rsqrt(ssq * (1.0 / HEAD_DIM) + EPS) * qg_ref[...]
        qa_o[:, hd * LANES:(hd + 1) * LANES] = (rope_a(xb) * scale).astype(BF16)

    ka = seg(col, LANES)
    col += LANES
    sq = ka * ka
    ssq_lo = jnp.sum(jnp.where(low, sq, 0.0), axis=-1, keepdims=True)
    ssq_hi = jnp.sum(jnp.where(low, 0.0, sq), axis=-1, keepdims=True)
    ssq = jnp.where(low, ssq_lo, ssq_hi)
    ka = ka * lax.rsqrt(ssq * (1.0 / HEAD_DIM) + EPS) * kg_ref[...]
    ka_o[...] = rope_a(ka).astype(BF16)

    va_o[...] = seg(col, LANES).astype(BF16)
    col += LANES

    qb = seg(col, QB_W)
    col += QB_W
    for hd in range(B_HEADS):
        xb = qb[:, hd * LANES:(hd + 1) * LANES]
        qb_o[:, hd * LANES:(hd + 1) * LANES] = (rope_b(xb) * scale).astype(BF16)

    kb_o[...] = rope_b(seg(col, LANES)).astype(BF16)
    col += LANES
    vb_o[...] = seg(col, LANES).astype(BF16)
    col += LANES

    qm_o[...] = (seg(col, QM_W) * scale).astype(BF16)
    col += QM_W

    gw = g_o.shape[1]
    g_o[...] = seg(col, gw).astype(BF16)


def _in_proj(x2d, gmix, w_pad, tabs, qg, kg, seq, tm):
    t, d = x2d.shape
    n_seq_tiles = seq // tm
    gate_w = w_pad.shape[1] - (QA_W + QB_W + QM_W + 4 * LANES)
    row = lambda width: pl.BlockSpec((tm, width), lambda i: (i, 0))
    const = lambda shape: pl.BlockSpec(shape, lambda i: (0, 0))
    tab = pl.BlockSpec((tm, LANES), lambda i: (i % n_seq_tiles, 0))
    widths = [QA_W, LANES, LANES, QB_W, LANES, LANES, QM_W, gate_w]
    return pl.pallas_call(
        _in_proj_kernel,
        grid=(t // tm,),
        in_specs=[row(d), const((1, d)), const(w_pad.shape), tab, tab, tab, tab,
                  const((1, LANES)), const((1, LANES))],
        out_specs=[row(w) for w in widths],
        out_shape=[jax.ShapeDtypeStruct((t, w), BF16) for w in widths],
        compiler_params=_cparams("arbitrary"),
        name="in_proj",
    )(x2d, gmix, w_pad, *tabs, qg, kg)


def _attn_a_kernel(q_ref, k_ref, v_ref, o_ref, *, tk):
    tq = q_ref.shape[0]
    nk = k_ref.shape[0] // tk
    for hd in range(A_HEADS):
        q = q_ref[:, hd * LANES:(hd + 1) * LANES]

        def body(c, carry):
            m, l, acc = carry
            off = pl.multiple_of(c * tk, tk)
            kc = k_ref[pl.ds(off, tk), :]
            vc = v_ref[pl.ds(off, tk), :]
            s = lax.dot_general(q, kc, (((1,), (1,)), ((), ())), preferred_element_type=F32)
            m_new = jnp.maximum(m, jnp.max(s, axis=-1, keepdims=True))
            p = jnp.exp(s - m_new)
            alpha = jnp.exp(m - m_new)
            l = alpha * l + jnp.sum(p, axis=-1, keepdims=True)
            acc = alpha * acc + jnp.dot(p.astype(BF16), vc, preferred_element_type=F32)
            return m_new, l, acc

        init = (jnp.full((tq, 1), -jnp.inf, F32), jnp.zeros((tq, 1), F32), jnp.zeros((tq, LANES), F32))
        _, l, acc = lax.fori_loop(0, nk, body, init)
        o_ref[:, hd * LANES:(hd + 1) * LANES] = (acc / l).astype(BF16)


def _attn_a(q, k, v, batch, seq, tq, tk):
    t = q.shape[0]
    nq = seq // tq
    qspec = pl.BlockSpec((tq, QA_W), lambda b, i: (b * nq + i, 0))
    kvspec = pl.BlockSpec((seq, LANES), lambda b, i: (b, 0))
    return pl.pallas_call(
        functools.partial(_attn_a_kernel, tk=tk),
        grid=(batch, nq),
        in_specs=[qspec, kvspec, kvspec],
        out_specs=qspec,
        out_shape=jax.ShapeDtypeStruct((t, QA_W), BF16),
        compiler_params=_cparams("arbitrary", "arbitrary"),
        name="attn_a",
    )(q, k, v)


def _attn_b_kernel(sink_ref, q_ref, k_ref, v_ref, o_ref):
    tq = q_ref.shape[0]
    seq = k_ref.shape[0]
    nblk = tq // WINDOW
    band = 3 * WINDOW
    i = pl.program_id(1)
    sink_col = sink_ref[...]
    for n in range(nblk):
        blk = i * nblk + n
        ws = pl.multiple_of(jnp.clip((blk - 1) * WINDOW, 0, seq - band), WINDOW)
        kw = k_ref[pl.ds(ws, band), :]
        vw = v_ref[pl.ds(ws, band), :]
        rows = slice(n * WINDOW, (n + 1) * WINDOW)
        qs = jnp.concatenate(
            [q_ref[rows, hd * LANES:(hd + 1) * LANES] for hd in range(B_HEADS)], axis=0)
        s = lax.dot_general(qs, kw, (((1,), (1,)), ((), ())), preferred_element_type=F32)
        qpos = blk * WINDOW + lax.broadcasted_iota(jnp.int32, (WINDOW, band), 0)
        kpos = ws + lax.broadcasted_iota(jnp.int32, (WINDOW, band), 1)
        bias = jnp.where(jnp.abs(kpos - qpos) <= WINDOW, 0.0, NEG)
        parts = []
        for hd in range(B_HEADS):
            hs = slice(hd * WINDOW, (hd + 1) * WINDOW)
            sh = jnp.where(bias < 0.0, NEG, s[hs])
            sk = sink_col[hs]
            m = jnp.maximum(jnp.max(sh, axis=-1, keepdims=True), sk)
            p = jnp.exp(sh - m)
            denom = jnp.sum(p, axis=-1, keepdims=True) + jnp.exp(sk - m)
            parts.append((p, denom))
        p_all = jnp.concatenate([p for p, _ in parts], axis=0).astype(BF16)
        o = jnp.dot(p_all, vw, preferred_element_type=F32)
        for hd in range(B_HEADS):
            oh = o[hd * WINDOW:(hd + 1) * WINDOW] / parts[hd][1]
            o_ref[rows, hd * LANES:(hd + 1) * LANES] = oh.astype(BF16)


def _attn_b(sink, q, k, v, batch, seq, tq):
    t = q.shape[0]
    nq = seq // tq
    qspec = pl.BlockSpec((tq, QB_W), lambda b, i: (b * nq + i, 0))
    kvspec = pl.BlockSpec((seq, LANES), lambda b, i: (b, 0))
    return pl.pallas_call(
        _attn_b_kernel,
        grid=(batch, nq),
        in_specs=[pl.BlockSpec(sink.shape, lambda b, i: (0, 0)), qspec, kvspec, kvspec],
        out_specs=qspec,
        out_shape=jax.ShapeDtypeStruct((t, QB_W), BF16),
        compiler_params=_cparams("arbitrary", "arbitrary"),
        name="attn_b",
    )(sink, q, k, v)


def _attn_m_kernel(q_ref, kv_ref, o_ref):
    kw = M_HEADS * HEAD_DIM
    for hd in range(M_HEADS):
        half = hd // 2
        q = q_ref[:, hd * LANES:(hd + 1) * LANES]
        k = kv_ref[:, half * LANES:(half + 1) * LANES]
        v = kv_ref[:, kw + half * LANES:kw + (half + 1) * LANES]
        s = lax.dot_general(q, k, (((1,), (1,)), ((), ())), preferred_element_type=F32)
        m = jnp.max(s, axis=-1, keepdims=True)
        p = jnp.exp(s - m)
        denom = jnp.sum(p, axis=-1, keepdims=True)
        o = jnp.dot(p.astype(BF16), v, preferred_element_type=F32) / denom
        o_ref[:, hd * LANES:(hd + 1) * LANES] = o.astype(BF16)


def _attn_m(q, kv, batch, seq, tq):
    t = q.shape[0]
    nq = seq // tq
    n_mem = kv.shape[0] // batch
    qspec = pl.BlockSpec((tq, QM_W), lambda b, i: (b * nq + i, 0))
    return pl.pallas_call(
        _attn_m_kernel,
        grid=(batch, nq),
        in_specs=[qspec, pl.BlockSpec((n_mem, kv.shape[1]), lambda b, i: (b, 0))],
        out_specs=qspec,
        out_shape=jax.ShapeDtypeStruct((t, QM_W), BF16),
        compiler_params=_cparams("arbitrary", "arbitrary"),
        name="attn_m",
    )(q, kv)


def _merge_kernel(x_ref, ya_ref, yb_ref, ym_ref, g_ref, wa_ref, wb_ref, wm_ref, wo_ref,
                  gffn_ref, wr_hi_ref, wr_lo_ref, br_ref,
                  x2_o, h2_o, idx_o, gate_o, rank_o, cnt_o, carry_ref):
    tm, d = x_ref.shape
    step = pl.program_id(0)

    @pl.when(step == 0)
    def _():
        carry_ref[...] = jnp.zeros_like(carry_ref)

    def branch(y_ref, w_ref, k):
        gate = jax.nn.sigmoid(g_ref[:, k * d:(k + 1) * d].astype(F32))
        return gate * jnp.dot(y_ref[...], w_ref[...], preferred_element_type=F32)

    merged = branch(ya_ref, wa_ref, 0) + branch(yb_ref, wb_ref, 1) + branch(ym_ref, wm_ref, 2)
    x2 = x_ref[...] + jnp.dot(merged.astype(BF16), wo_ref[...], preferred_element_type=F32)
    x2_o[...] = x2
    h2 = _rms(x2, gffn_ref[...])
    h2_o[...] = h2

    h_hi = h2.astype(BF16)
    h_lo = (h2 - h_hi.astype(F32)).astype(BF16)
    logits = (jnp.dot(h_hi, wr_hi_ref[...], preferred_element_type=F32)
              + jnp.dot(h_lo, wr_hi_ref[...], preferred_element_type=F32)
              + jnp.dot(h_hi, wr_lo_ref[...], preferred_element_type=F32)) + br_ref[...]

    lane = lax.broadcasted_iota(jnp.int32, (tm, LANES), 1)
    lane_f = lane.astype(F32)
    work = logits
    vals, hots = [], []
    idx_out = jnp.zeros((tm, LANES), F32)
    for k in range(TOP_K):
        mx = jnp.max(work, axis=-1, keepdims=True)
        idx = jnp.min(jnp.where(work == mx, lane_f, float(LANES)), axis=-1, keepdims=True)
        hot = lane_f == idx
        work = jnp.where(hot, NEG, work)
        vals.append(mx)
        hots.append(hot)
        idx_out = jnp.where(lane == k, idx, idx_out)
    idx_o[...] = idx_out.astype(jnp.int32)

    exps = [jnp.exp(v - vals[0]) for v in vals]
    denom = exps[0] + exps[1] + exps[2] + exps[3]
    gate_out = jnp.zeros((tm, LANES), F32)
    for k in range(TOP_K):
        gate_out = jnp.where(lane == k, exps[k] / denom, gate_out)
    gate_o[...] = gate_out

    cnt = jnp.zeros((tm, LANES), F32)
    for k in range(TOP_K):
        cnt = cnt + jnp.where(hots[k], 1.0, 0.0)
    r_i = lax.broadcasted_iota(jnp.int32, (tm, tm), 0)
    c_i = lax.broadcasted_iota(jnp.int32, (tm, tm), 1)
    tril = jnp.where(c_i < r_i, 1.0, 0.0).astype(BF16)
    before = jnp.dot(tril, cnt.astype(BF16), preferred_element_type=F32) + carry_ref[...]
    rank_out = jnp.zeros((tm, LANES), jnp.int32)
    for k in range(TOP_K):
        rk = jnp.sum(jnp.where(hots[k], before, 0.0), axis=-1, keepdims=True).astype(jnp.int32)
        rank_out = jnp.where(lane == k, rk, rank_out)
    rank_o[...] = rank_out
    carry_ref[...] = carry_ref[...] + jnp.sum(cnt, axis=0, keepdims=True)
    cnt_o[...] = carry_ref[...]


def _merge(x2d, ya, yb, ym, gates, wa, wb, wm, wo, gffn, wr_hi, wr_lo, br, tm):
    t, d = x2d.shape
    row = lambda width: pl.BlockSpec((tm, width), lambda i: (i, 0))
    const = lambda a: pl.BlockSpec(a.shape, lambda i: (0, 0))
    outs = [((t, d), F32), ((t, d), F32), ((t, LANES), jnp.int32), ((t, LANES), F32),
            ((t, LANES), jnp.int32), ((1, LANES), F32)]
    return pl.pallas_call(
        _merge_kernel,
        grid=(t // tm,),
        in_specs=[row(d), row(ya.shape[1]), row(yb.shape[1]), row(ym.shape[1]), row(gates.shape[1]),
                  const(wa), const(wb), const(wm), const(wo), const(gffn), const(wr_hi), const(wr_lo),
                  const(br)],
        out_specs=[row(d), row(d), row(LANES), row(LANES), row(LANES),
                   pl.BlockSpec((1, LANES), lambda i: (0, 0))],
        out_shape=[jax.ShapeDtypeStruct(s, dt) for s, dt in outs],
        scratch_shapes=[pltpu.VMEM((1, LANES), F32)],
        compiler_params=_cparams("arbitrary"),
        name="merge_router",
    )(x2d, ya, yb, ym, gates, wa, wb, wm, wo, gffn, wr_hi, wr_lo, br)


def _dispatch_kernel(dest_ref, h_ref, xs_in_ref, xs_ref, sem):
    del xs_in_ref
    tm = h_ref.shape[0]
    base = pl.program_id(0) * tm * TOP_K

    def row_copy(r, j):
        d = dest_ref[base + r * TOP_K + j]
        return pltpu.make_async_copy(h_ref.at[pl.ds(r, 1)], xs_ref.at[pl.ds(d, 1)], sem)

    def issue(r, c):
        for j in range(TOP_K):
            row_copy(r, j).start()
        return c

    def drain(r, c):
        for j in range(TOP_K):
            row_copy(r, j).wait()
        return c

    lax.fori_loop(0, tm, issue, 0)
    lax.fori_loop(0, tm, drain, 0)


def _dispatch(dest_flat, h2, xs_zero, tm):
    t, d = h2.shape
    return pl.pallas_call(
        _dispatch_kernel,
        grid_spec=pltpu.PrefetchScalarGridSpec(
            num_scalar_prefetch=1,
            grid=(t // tm,),
            in_specs=[pl.BlockSpec((tm, d), lambda i, dest: (i, 0)),
                      pl.BlockSpec(memory_space=pl.ANY)],
            out_specs=pl.BlockSpec(memory_space=pl.ANY),
            scratch_shapes=[pltpu.SemaphoreType.DMA],
        ),
        out_shape=jax.ShapeDtypeStruct(xs_zero.shape, xs_zero.dtype),
        input_output_aliases={2: 0},
        compiler_params=_cparams("arbitrary"),
        name="dispatch",
    )(dest_flat, h2, xs_zero)


def _experts_kernel(blk_e_ref, n_used_ref, x_ref, wgu_ref, bgu_ref, wd_ref, bd_ref, o_ref,
                    wgu_bf, wd_bf):
    j = pl.program_id(0)
    f = wd_ref.shape[1]
    prev = blk_e_ref[jnp.maximum(j - 1, 0)]
    new_expert = jnp.logical_or(j == 0, blk_e_ref[j] != prev)

    @pl.when(jnp.logical_and(new_expert, j < n_used_ref[0]))
    def _():
        def cast_rows(src, dst):
            def body(c, carry):
                rows = pl.ds(pl.multiple_of(c * CAST_ROWS, CAST_ROWS), CAST_ROWS)
                dst[rows, :] = src[0, rows, :].astype(BF16)
                return carry
            lax.fori_loop(0, src.shape[1] // CAST_ROWS, body, 0)

        cast_rows(wgu_ref, wgu_bf)
        cast_rows(wd_ref, wd_bf)

    @pl.when(j < n_used_ref[0])
    def _():
        xb = x_ref[...].astype(BF16)
        gu = jnp.dot(xb, wgu_bf[...], preferred_element_type=F32) + bgu_ref[0]
        g = jnp.minimum(gu[:, :f], SWIGLU_LIMIT)
        lin = jnp.clip(gu[:, f:], -SWIGLU_LIMIT, SWIGLU_LIMIT)
        act = g * jax.nn.sigmoid(SWIGLU_ALPHA * g) * (lin + 1.0)
        o_ref[...] = jnp.dot(act.astype(BF16), wd_bf[...], preferred_element_type=F32) + bd_ref[0]

    @pl.when(j >= n_used_ref[0])
    def _():
        o_ref[...] = jnp.zeros_like(o_ref)


def _experts(blk_e, n_used, xs, wgu, bgu, wd, bd, bm):
    p, d = xs.shape
    n_e, _, f2 = wgu.shape
    f = f2 // 2
    last = lambda j, be, nu: jnp.minimum(j, nu[0] - 1)
    return pl.pallas_call(
        _experts_kernel,
        grid_spec=pltpu.PrefetchScalarGridSpec(
            num_scalar_prefetch=2,
            grid=(p // bm,),
            in_specs=[pl.BlockSpec((bm, d), lambda j, be, nu: (last(j, be, nu), 0)),
                      pl.BlockSpec((1, d, f2), lambda j, be, nu: (be[last(j, be, nu)], 0, 0)),
                      pl.BlockSpec((1, 1, f2), lambda j, be, nu: (be[last(j, be, nu)], 0, 0)),
                      pl.BlockSpec((1, f, d), lambda j, be, nu: (be[last(j, be, nu)], 0, 0)),
                      pl.BlockSpec((1, 1, d), lambda j, be, nu: (be[last(j, be, nu)], 0, 0))],
            out_specs=pl.BlockSpec((bm, d), lambda j, be, nu: (j, 0)),
            scratch_shapes=[pltpu.VMEM((d, f2), BF16), pltpu.VMEM((f, d), BF16)],
        ),
        out_shape=jax.ShapeDtypeStruct((p, d), F32),
        compiler_params=_cparams("arbitrary"),
        name="experts",
    )(blk_e, n_used, xs, wgu, bgu.reshape(n_e, 1, f2), wd, bd.reshape(n_e, 1, d))


def _combine_kernel(dest_ref, x2_ref, gate_ref, gfin_ref, ys_ref, o_ref, buf, sem):
    tm = x2_ref.shape[0]
    base = pl.program_id(0) * tm * TOP_K

    def row_copy(r, j):
        d = dest_ref[base + r * TOP_K + j]
        return pltpu.make_async_copy(ys_ref.at[pl.ds(d, 1)], buf.at[j, pl.ds(r, 1)], sem)

    def issue(r, c):
        for j in range(TOP_K):
            row_copy(r, j).start()
        return c

    def drain(r, c):
        for j in range(TOP_K):
            row_copy(r, j).wait()
        return c

    lax.fori_loop(0, tm, issue, 0)
    lax.fori_loop(0, tm, drain, 0)

    gate = gate_ref[...]
    x3 = x2_ref[...]
    for j in range(TOP_K):
        x3 = x3 + gate[:, j:j + 1] * buf[j]
    o_ref[...] = _rms(x3, gfin_ref[...])


def _combine(dest_flat, x2, gate, gfin, ys, tm):
    t, d = x2.shape
    return pl.pallas_call(
        _combine_kernel,
        grid_spec=pltpu.PrefetchScalarGridSpec(
            num_scalar_prefetch=1,
            grid=(t // tm,),
            in_specs=[pl.BlockSpec((tm, d), lambda i, dest: (i, 0)),
                      pl.BlockSpec((tm, LANES), lambda i, dest: (i, 0)),
                      pl.BlockSpec((1, d), lambda i, dest: (0, 0)),
                      pl.BlockSpec(memory_space=pl.ANY)],
            out_specs=pl.BlockSpec((tm, d), lambda i, dest: (i, 0)),
            scratch_shapes=[pltpu.VMEM((TOP_K, tm, d), F32), pltpu.SemaphoreType.DMA],
        ),
        out_shape=jax.ShapeDtypeStruct((t, d), F32),
        compiler_params=_cparams("arbitrary"),
        name="combine",
    )(dest_flat, x2, gate, gfin, ys)


def _place_heads(w, n_heads, slot_of_head, axis):
    hot = jnp.array([[1.0 if slot_of_head(h) == s else 0.0 for s in range(2)]
                     for h in range(n_heads)], w.dtype)
    if axis == 1:
        rows = w.shape[0]
        w3 = w.reshape(rows, n_heads, 1, HEAD_DIM) * hot[None, :, :, None]
        return w3.reshape(rows, n_heads * LANES)
    cols = w.shape[1]
    w3 = w.reshape(n_heads, 1, HEAD_DIM, cols) * hot[:, :, None, None]
    return w3.reshape(n_heads * LANES, cols)


def _rope_tables(seq):
    pos = jnp.arange(seq, dtype=jnp.int32)
    rows = (pos // GRID_W).astype(F32)
    cols = (pos % GRID_W).astype(F32)

    def angles(p, dim):
        inv = ROPE_THETA ** (-jnp.arange(0, dim, 2, dtype=F32) / dim)
        return p[:, None] * inv[None, :]

    ang_r, ang_c = angles(rows, 32), angles(cols, 32)
    ang_a = jnp.concatenate([ang_r, ang_r, ang_c, ang_c], axis=-1)
    sign_a = jnp.tile(jnp.concatenate([-jnp.ones(16, F32), jnp.ones(16, F32)]), 2)
    ang_p = angles(pos.astype(F32), HEAD_DIM)
    ang_b = jnp.concatenate([ang_p, ang_p], axis=-1)
    sign_b = jnp.concatenate([-jnp.ones(32, F32), jnp.ones(32, F32)])
    two = lambda a: jnp.tile(a, (1, 2))
    return (two(jnp.cos(ang_a)), two(jnp.sin(ang_a) * sign_a),
            two(jnp.cos(ang_b)), two(jnp.sin(ang_b) * sign_b))


def _layer(x2d, mem2d, batch, seq, p, tiles):
    t, d = x2d.shape
    aq, akv = A_HEADS * HEAD_DIM, A_KV * HEAD_DIM
    bq, bkv = B_HEADS * HEAD_DIM, B_KV * HEAD_DIM
    mq = M_HEADS * HEAD_DIM
    w_in = p["w_in"]
    cuts = [0]
    for wdt in (aq, akv, akv, bq, bkv, bkv, mq):
        cuts.append(cuts[-1] + wdt)
    wq_a, wk_a, wv_a, wq_b, wk_b, wv_b, wq_m = (w_in[:, cuts[i]:cuts[i + 1]] for i in range(7))
    w_gates = w_in[:, cuts[-1]:]
    kv_of_a = lambda h: h // (A_HEADS // A_KV)
    kv_of_b = lambda h: h // (B_HEADS // B_KV)
    half_of_m = lambda h: h % 2
    w_pad = jnp.concatenate([
        _place_heads(wq_a, A_HEADS, kv_of_a, 1), wk_a, wv_a,
        _place_heads(wq_b, B_HEADS, kv_of_b, 1), wk_b, wv_b,
        _place_heads(wq_m, M_HEADS, half_of_m, 1), w_gates], axis=1).astype(BF16)
    wa = _place_heads(p["w_branch_a"], A_HEADS, kv_of_a, 0).astype(BF16)
    wb = _place_heads(p["w_branch_b"], B_HEADS, kv_of_b, 0).astype(BF16)
    wm = _place_heads(p["w_branch_m"], M_HEADS, half_of_m, 0).astype(BF16)
    two = lambda g: jnp.tile(g.reshape(1, HEAD_DIM), (1, 2))

    kv_m = _mem_kv(mem2d, p["norm_mem"].reshape(1, d), p["w_mem_kv"].astype(BF16))
    qa, ka, va, qb, kb, vb, qm, gates = _in_proj(
        x2d, p["norm_mix"].reshape(1, d), w_pad, _rope_tables(seq),
        two(p["q_norm_a"]), two(p["k_norm_a"]), seq, tiles["proj"])
    ya = _attn_a(qa, ka, va, batch, seq, tiles["attn_q"], tiles["attn_k"])
    sink_rows = jnp.repeat(p["sink_b"].astype(F32), WINDOW).reshape(B_HEADS * WINDOW, 1)
    yb = _attn_b(sink_rows, qb, kb, vb, batch, seq, tiles["win_q"])
    ym = _attn_m(qm, kv_m, batch, seq, tiles["mem_q"])

    wr = jnp.zeros((d, LANES), F32).at[:, :N_EXPERTS].set(p["w_router"])
    wr_hi = wr.astype(BF16)
    wr_lo = (wr - wr_hi.astype(F32)).astype(BF16)
    br = jnp.full((1, LANES), NEG, F32).at[0, :N_EXPERTS].set(p["b_router"])
    x2, h2, idx, gate, rank, counts = _merge(
        x2d, ya, yb, ym, gates, wa, wb, wm, p["w_out"].astype(BF16), p["norm_ffn"].reshape(1, d),
        wr_hi, wr_lo, br, tiles["merge"])

    bm = tiles["expert_rows"]
    counts = counts[0, :N_EXPERTS].astype(jnp.int32)
    padded = (counts + bm - 1) // bm * bm
    pend = jnp.cumsum(padded)
    pstart = pend - padded
    dest = pstart[idx[:, :TOP_K]] + rank[:, :TOP_K]
    dest_flat = dest.reshape(t * TOP_K).astype(jnp.int32)
    p_rows = t * TOP_K + N_EXPERTS * bm
    nblk = p_rows // bm
    blk_e = jnp.minimum(jnp.searchsorted(pend, jnp.arange(nblk, dtype=jnp.int32) * bm, side="right"),
                        N_EXPERTS - 1).astype(jnp.int32)
    n_used = (pend[-1:] // bm).astype(jnp.int32)

    xs = _dispatch(dest_flat, h2, jnp.zeros((p_rows, d), F32), tiles["dispatch"])
    ys = _experts(blk_e, n_used, xs, p["w_gate_up"], p["b_gate_up"], p["w_down"], p["b_down"], bm)
    return dest_flat, x2, gate, ys


def kernel(x, mem, norm_mix, w_in, q_norm_a, k_norm_a, sink_b, norm_mem, w_mem_kv, w_branch_a,
           w_branch_b, w_branch_m, w_out, norm_ffn, w_router, b_router, w_gate_up, b_gate_up, w_down,
           b_down, norm_final):
    batch, seq, d = x.shape
    depth = norm_mix.shape[0]
    assert depth == 1, "the combine step fuses the final norm, which assumes a single layer"
    t = batch * seq
    tiles = dict(proj=min(256, seq), attn_q=min(256, seq), attn_k=min(1024, seq),
                 win_q=min(512, seq), mem_q=min(512, seq), merge=min(256, seq),
                 expert_rows=256, dispatch=min(512, seq), combine=min(256, seq))
    x2d = x.reshape(t, d)
    mem2d = mem.reshape(-1, d)
    layer = 0
    params = dict(norm_mix=norm_mix[layer], w_in=w_in[layer], q_norm_a=q_norm_a[layer],
                  k_norm_a=k_norm_a[layer], sink_b=sink_b[layer], norm_mem=norm_mem[layer],
                  w_mem_kv=w_mem_kv[layer], w_branch_a=w_branch_a[layer], w_branch_b=w_branch_b[layer],
                  w_branch_m=w_branch_m[layer], w_out=w_out[layer], norm_ffn=norm_ffn[layer],
                  w_router=w_router[layer], b_router=b_router[layer], w_gate_up=w_gate_up[layer],
                  b_gate_up=b_gate_up[layer], w_down=w_down[layer], b_down=b_down[layer])
    dest_flat, x2, gate, ys = _layer(x2d, mem2d, batch, seq, params, tiles)
    out = _combine(dest_flat, x2, gate, norm_final.reshape(1, d), ys, tiles["combine"])
    return out.reshape(batch, seq, d)
```
